```python
import math
import jax, jax.numpy as jnp
from jax import lax
import numpy as np

D_MODEL = 1024
BATCH = 8
SEQ = 2048
DEPTH = 2
DEC_BATCH = 128
DEC_SEQ = 8
PAST_LEN = 16384
PAGE_SIZE = 128

D_CONV = D_MODEL
CONV_A_W = 3
D_RNN = D_MODEL
CONV_B_W = 4
N_BLK = 16
BLK = D_RNN // N_BLK
LRU_C = 8.0
N_MEM = 256
MEM_HEADS = 4
MEM_HEAD_DIM = D_MODEL // MEM_HEADS
D_FF = int(math.ceil(8 * D_MODEL / 3 / 256) * 256)
N_NORMS = 7
EPS = 1e-6
D_IN = 3 * D_CONV + D_RNN + 2 * D_MODEL

kernel_name = "hybrid_conv_rglru_xattn_decoder_step"


def rms_norm(x, g):
    xf = x.astype(jnp.float32)
    y = xf * lax.rsqrt(jnp.mean(xf * xf, axis=-1, keepdims=True) + EPS)
    return (y * g.astype(jnp.float32)).astype(x.dtype)


def causal_dwconv(x, buf, w):
    width = w.shape[0]
    t = x.shape[1]
    xp = jnp.concatenate([buf.astype(x.dtype), x], axis=1)
    y = xp[:, 0:t] * w[0]
    for k in range(1, width):
        y = y + xp[:, k:k + t] * w[k]
    return y, xp[:, t:]


def rg_lru(u, h0, w_a, b_a, w_x, b_x, lam):
    bsz, t, c = u.shape
    ub = u.reshape(bsz, t, N_BLK, BLK)
    r = jax.nn.sigmoid(jnp.einsum('btnk,nkj->btnj', ub, w_a).reshape(bsz, t, c) + b_a)
    i = jax.nn.sigmoid(jnp.einsum('btnk,nkj->btnj', ub, w_x).reshape(bsz, t, c) + b_x)
    log_a = -LRU_C * r.astype(jnp.float32) * jax.nn.softplus(-lam.astype(jnp.float32))
    a = jnp.exp(log_a)
    mult = jnp.sqrt(-jnp.expm1(2.0 * log_a))
    b = mult * (i * u).astype(jnp.float32)
    b = b.at[:, 0].add(a[:, 0] * h0.astype(jnp.float32))

    def combine(c1, c2):
        a1, b1 = c1
        a2, b2 = c2
        return a1 * a2, a2 * b1 + b2

    _, h = lax.associative_scan(combine, (a, b), axis=1)
    return h.astype(u.dtype), h[:, -1]


def mem_kv(mem, g_mem, w_kv):
    m = rms_norm(mem, g_mem)
    kv = (m @ w_kv).reshape(mem.shape[0], N_MEM, 2, MEM_HEADS, MEM_HEAD_DIM)
    return kv[:, :, 0], kv[:, :, 1]


def cross_attend(xn, mk, mv, w_q, w_o):
    bsz, t, _ = xn.shape
    q = (xn @ w_q).reshape(bsz, t, MEM_HEADS, MEM_HEAD_DIM)
    s = jnp.einsum('bthd,bmhd->bhtm', q.astype(jnp.float32), mk.astype(jnp.float32)) * (MEM_HEAD_DIM ** -0.5)
    p = jax.nn.softmax(s, axis=-1).astype(xn.dtype)
    o = jnp.einsum('bhtm,bmhd->bthd', p, mv.astype(xn.dtype)).reshape(bsz, t, D_MODEL)
    return o @ w_o


def layer(x, mk, mv, buf_a, buf_b, h0, lw):
    (g, w_in, conv_a_w, w_conv_out, conv_b_w, conv_b_b, w_gate_a, b_gate_a, w_gate_x, b_gate_x,
     lru_lambda, w_rnn_out, w_mix_out, w_q_x, w_o_x, w_ffn_in, w_ffn_out) = lw
    xn = rms_norm(x, g[0])
    proj = xn @ w_in
    cuts = [D_CONV, 2 * D_CONV, 3 * D_CONV, 3 * D_CONV + D_RNN, 3 * D_CONV + D_RNN + D_MODEL]
    hb, hc, hh, u, gc, gr = jnp.split(proj, cuts, axis=-1)
    ya, new_a = causal_dwconv(hc * hh, buf_a, conv_a_w)
    y_conv = (hb * ya) @ w_conv_out
    uc, new_b = causal_dwconv(u, buf_b, conv_b_w)
    uc = uc + conv_b_b
    hseq, h_last = rg_lru(uc, h0, w_gate_a, b_gate_a, w_gate_x, b_gate_x, lru_lambda)
    y_rnn = hseq @ w_rnn_out
    z = jax.nn.sigmoid(gc) * y_conv + jax.nn.sigmoid(gr) * y_rnn
    x = x + rms_norm(z @ w_mix_out, g[1])
    x = x + rms_norm(cross_attend(rms_norm(x, g[2]), mk, mv, w_q_x, w_o_x), g[3])
    xn = rms_norm(x, g[4])
    gate, up = jnp.split(xn @ w_ffn_in, 2, axis=-1)
    x = x + rms_norm((jax.nn.silu(gate) * up) @ w_ffn_out, g[5])
    return x, new_a, new_b, h_last


def setup_inputs(seed: int = 0) -> dict:
    key = jax.random.key(seed)
    ks = jax.random.split(key, 32)
    f32 = jnp.float32
    nrm = lambda k, shape, s: (jax.random.normal(k, shape, f32) * s)
    a_init = jax.random.uniform(ks[20], (DEPTH, D_RNN), f32, 0.9, 0.999)
    return {
        "x_prompt": nrm(ks[0], (BATCH, SEQ, D_MODEL), 1.0),
        "x_sample": nrm(ks[1], (DEC_BATCH, DEC_SEQ, D_MODEL), 1.0),
        "state_conv_a": nrm(ks[2], (DEPTH, DEC_BATCH, CONV_A_W - 1, D_CONV), 1.0),
        "state_conv_b": nrm(ks[3], (DEPTH, DEC_BATCH, CONV_B_W - 1, D_RNN), 1.0),
        "state_rglru": nrm(ks[4], (DEPTH, DEC_BATCH, D_RNN), 0.5),
        "cache_mem_k": nrm(ks[5], (DEPTH, DEC_BATCH, N_MEM, MEM_HEADS, MEM_HEAD_DIM), 1.0),
        "cache_mem_v": nrm(ks[6], (DEPTH, DEC_BATCH, N_MEM, MEM_HEADS, MEM_HEAD_DIM), 1.0),
        "mem_prompt": nrm(ks[7], (BATCH, N_MEM, D_MODEL), 1.0),
        "norm_gains": 1.0 + nrm(ks[8], (DEPTH, N_NORMS, D_MODEL), 0.05),
        "w_in": nrm(ks[9], (DEPTH, D_MODEL, D_IN), D_MODEL ** -0.5),
        "conv_a_w": nrm(ks[10], (DEPTH, CONV_A_W, D_CONV), CONV_A_W ** -0.5),
        "w_conv_out": nrm(ks[11], (DEPTH, D_CONV, D_MODEL), D_CONV ** -0.5),
        "conv_b_w": nrm(ks[12], (DEPTH, CONV_B_W, D_RNN), CONV_B_W ** -0.5),
        "conv_b_b": nrm(ks[13], (DEPTH, D_RNN), 0.01),
        "w_gate_a": nrm(ks[14], (DEPTH, N_BLK, BLK, BLK), BLK ** -0.5),
        "b_gate_a": nrm(ks[15], (DEPTH, D_RNN), 0.01),
        "w_gate_x": nrm(ks[16], (DEPTH, N_BLK, BLK, BLK), BLK ** -0.5),
        "b_gate_x": nrm(ks[17], (DEPTH, D_RNN), 0.01),
        "lru_lambda": jnp.log(a_init) - jnp.log1p(-a_init),
        "w_rnn_out": nrm(ks[18], (DEPTH, D_RNN, D_MODEL), D_RNN ** -0.5),
        "w_mix_out": nrm(ks[19], (DEPTH, D_MODEL, D_MODEL), D_MODEL ** -0.5),
        "w_kv_x": nrm(ks[21], (DEPTH, D_MODEL, 2 * D_MODEL), D_MODEL ** -0.5),
        "w_q_x": nrm(ks[22], (DEPTH, D_MODEL, D_MODEL), D_MODEL ** -0.5),
        "w_o_x": nrm(ks[23], (DEPTH, D_MODEL, D_MODEL), D_MODEL ** -0.5),
        "w_ffn_in": nrm(ks[24], (DEPTH, D_MODEL, 2 * D_FF), D_MODEL ** -0.5),
        "w_ffn_out": nrm(ks[25], (DEPTH, D_FF, D_MODEL), D_FF ** -0.5),
    }


def reference(x_prompt, x_sample, state_conv_a, state_conv_b, state_rglru, cache_mem_k, cache_mem_v,
              mem_prompt, norm_gains, w_in, conv_a_w, w_conv_out, conv_b_w, conv_b_b, w_gate_a, b_gate_a,
              w_gate_x, b_gate_x, lru_lambda, w_rnn_out, w_mix_out, w_kv_x, w_q_x, w_o_x, w_ffn_in, w_ffn_out):
    xp, xs = x_prompt, x_sample
    bsz = xp.shape[0]
    zero_a = jnp.zeros((bsz, CONV_A_W - 1, D_CONV), xp.dtype)
    zero_b = jnp.zeros((bsz, CONV_B_W - 1, D_RNN), xp.dtype)
    zero_h = jnp.zeros((bsz, D_RNN), jnp.float32)
    pa, pb, ph, pk, pv, sa, sb, sh = [], [], [], [], [], [], [], []
    for l in range(DEPTH):
        lw = (norm_gains[l], w_in[l], conv_a_w[l], w_conv_out[l], conv_b_w[l], conv_b_b[l], w_gate_a[l],
              b_gate_a[l], w_gate_x[l], b_gate_x[l], lru_lambda[l], w_rnn_out[l], w_mix_out[l], w_q_x[l],
              w_o_x[l], w_ffn_in[l], w_ffn_out[l])
        mk, mv = mem_kv(mem_prompt, norm_gains[l, 6], w_kv_x[l])
        xp, na, nb, nh = layer(xp, mk, mv, zero_a, zero_b, zero_h, lw)
        pa.append(na); pb.append(nb); ph.append(nh); pk.append(mk); pv.append(mv)
        xs, ma, mb, mh = layer(xs, cache_mem_k[l], cache_mem_v[l], state_conv_a[l], state_conv_b[l],
                               state_rglru[l], lw)
        sa.append(ma); sb.append(mb); sh.append(mh)
    return (xp, xs, jnp.stack(pa), jnp.stack(pb), jnp.stack(ph), jnp.stack(pk), jnp.stack(pv),
            jnp.stack(sa), jnp.stack(sb), jnp.stack(sh))
```

```python
import math

import jax
import jax.numpy as jnp
from jax import lax
from jax.experimental import pallas as pl
from jax.experimental.pallas import tpu as pltpu

D_MODEL = 1024
N_MEM = 256
MEM_HEADS = 4
MEM_HEAD_DIM = D_MODEL // MEM_HEADS
N_BLK = 16
BLK = D_MODEL // N_BLK
LRU_C = 8.0
EPS = 1e-6
D_FF = int(math.ceil(8 * D_MODEL / 3 / 256) * 256)
N_PROJ = 6

SUBLANES = 8
MXU_DIM = 256
GATE_CHUNKS = D_MODEL // MXU_DIM
BLK_PER_CHUNK = MXU_DIM // BLK
VMEM_LIMIT = 56 * 1024 * 1024

TM_PROMPT = 512
SB_MIX = 64
SB_ATTN = 4
TM_FFN = 512
TM_MEM = 512

BF16 = jnp.bfloat16
F32 = jnp.float32


def _rms(x, g):
    ms = jnp.mean(x * x, axis=-1, keepdims=True)
    return x * lax.rsqrt(ms + EPS) * g


def _bdot(a, w):
    return jnp.dot(a.astype(BF16), w, preferred_element_type=F32)


def _layer_spec(block, layer, tail=None):
    idx = (layer,) + (tail if tail is not None else (0,) * (len(block) - 1))
    return pl.BlockSpec(block, lambda *_: idx, pipeline_mode=pl.Buffered(1))


def _gain_spec(layer, k):
    return _layer_spec((None, None, 1, D_MODEL), layer, (k, 0, 0))


def _params(n_axes):
    return pltpu.CompilerParams(dimension_semantics=("arbitrary",) * n_axes,
                                vmem_limit_bytes=VMEM_LIMIT)


def _group_scan(a, b):
    t = lax.broadcasted_iota(jnp.int32, (1, SUBLANES, a.shape[-1]), 1)
    d = 1
    while d < SUBLANES:
        keep = t >= d
        a_sh = jnp.where(keep, pltpu.roll(a, d, axis=1), 1.0)
        b_sh = jnp.where(keep, pltpu.roll(b, d, axis=1), 0.0)
        b = a * b_sh + b
        a = a * a_sh
        d *= 2
    return a, b


def _lru_coeffs(uc, wga_ref, bga_ref, wgx_ref, bgx_ref, lam_ref):
    ucb = uc.astype(BF16)
    r_parts, i_parts = [], []
    for c in range(GATE_CHUNKS):
        sl = slice(c * MXU_DIM, (c + 1) * MXU_DIM)
        r_parts.append(jnp.dot(ucb[:, sl], wga_ref[c], preferred_element_type=F32))
        i_parts.append(jnp.dot(ucb[:, sl], wgx_ref[c], preferred_element_type=F32))
    r = jax.nn.sigmoid(jnp.concatenate(r_parts, axis=-1) + bga_ref[...])
    i = jax.nn.sigmoid(jnp.concatenate(i_parts, axis=-1) + bgx_ref[...])
    lam = lam_ref[...]
    softplus_neg_lam = jnp.maximum(-lam, 0.0) + jnp.log1p(jnp.exp(-jnp.abs(lam)))
    log_a = (-LRU_C * r) * softplus_neg_lam
    a = jnp.exp(log_a)
    mult = jnp.sqrt(-jnp.tanh(log_a) * (1.0 + a * a))
    return a, mult * (i * uc)


def _proj(xnb, win_ref, k):
    return jnp.dot(xnb, win_ref[:, k * D_MODEL:(k + 1) * D_MODEL], preferred_element_type=F32)


def _mix_tail(x, xnb, ya, hseq, win_ref, g1_ref, wco_ref, wro_ref, wmo_ref):
    y_conv = _bdot(_proj(xnb, win_ref, 0) * ya, wco_ref[...])
    y_rnn = _bdot(hseq, wro_ref[...])
    z = jax.nn.sigmoid(_proj(xnb, win_ref, 4)) * y_conv + jax.nn.sigmoid(_proj(xnb, win_ref, 5)) * y_rnn
    return x + _rms(_bdot(z, wmo_ref[...]), g1_ref[...])


def _mix_prompt_kernel(x_ref, g0_ref, g1_ref, win_ref, caw_ref, wco_ref, cbw_ref, cbb_ref,
                       wga_ref, bga_ref, wgx_ref, bgx_ref, lam_ref, wro_ref, wmo_ref,
                       y_ref, na_ref, nb_ref, nh_ref,
                       ga_scr, ub_scr, h_scr):
    t_idx = pl.program_id(1)
    tm = x_ref.shape[0]
    groups = tm // SUBLANES

    @pl.when(t_idx == 0)
    def _():
        ga_scr[0:SUBLANES, :] = jnp.zeros((SUBLANES, D_MODEL), F32)
        ub_scr[0:SUBLANES, :] = jnp.zeros((SUBLANES, D_MODEL), F32)
        h_scr[...] = jnp.zeros_like(h_scr)

    x = x_ref[...]
    xnb = _rms(x, g0_ref[...]).astype(BF16)

    g = _proj(xnb, win_ref, 1) * _proj(xnb, win_ref, 2)
    ga_scr[SUBLANES:, :] = g
    ya = (ga_scr[pl.ds(SUBLANES - 2, tm), :] * caw_ref[0:1, :]
          + ga_scr[pl.ds(SUBLANES - 1, tm), :] * caw_ref[1:2, :]
          + g * caw_ref[2:3, :])
    ga_scr[0:SUBLANES, :] = ga_scr[pl.ds(tm, SUBLANES), :]

    u = _proj(xnb, win_ref, 3)
    ub_scr[SUBLANES:, :] = u
    uc = (ub_scr[pl.ds(SUBLANES - 3, tm), :] * cbw_ref[0:1, :]
          + ub_scr[pl.ds(SUBLANES - 2, tm), :] * cbw_ref[1:2, :]
          + ub_scr[pl.ds(SUBLANES - 1, tm), :] * cbw_ref[2:3, :]
          + u * cbw_ref[3:4, :]) + cbb_ref[...]
    ub_scr[0:SUBLANES, :] = ub_scr[pl.ds(tm, SUBLANES), :]

    a, b = _lru_coeffs(uc, wga_ref, bga_ref, wgx_ref, bgx_ref, lam_ref)
    a, b = _group_scan(a.reshape(groups, SUBLANES, D_MODEL), b.reshape(groups, SUBLANES, D_MODEL))
    carry = h_scr[...]
    h_groups = []
    for j in range(groups):
        hj = b[j] + a[j] * carry
        h_groups.append(hj)
        carry = jnp.broadcast_to(hj[SUBLANES - 1:SUBLANES, :], (SUBLANES, D_MODEL))
    h_scr[...] = carry
    hseq = jnp.concatenate(h_groups, axis=0)

    y_ref[...] = _mix_tail(x, xnb, ya, hseq, win_ref, g1_ref, wco_ref, wro_ref, wmo_ref)

    @pl.when(t_idx == pl.num_programs(1) - 1)
    def _():
        na_ref[...] = ga_scr[pl.ds(SUBLANES - 2, 2), :]
        nb_ref[...] = ub_scr[pl.ds(SUBLANES - 3, 3), :]
        nh_ref[...] = h_scr[0:1, :]


def _shift_with_state(v, st_ref, s):
    n_state = st_ref.shape[1]
    t = lax.broadcasted_iota(jnp.int32, (1, SUBLANES, v.shape[-1]), 1)
    out = pltpu.roll(v, s, axis=1)
    for k in range(s):
        row = n_state - s + k
        out = jnp.where(t == k, st_ref[:, row:row + 1, :], out)
    return out


def _mix_sample_kernel(x_ref, sa_ref, sb_ref, sh_ref, g0_ref, g1_ref, win_ref, caw_ref, wco_ref,
                       cbw_ref, cbb_ref, wga_ref, bga_ref, wgx_ref, bgx_ref, lam_ref, wro_ref, wmo_ref,
                       y_ref, na_ref, nb_ref, nh_ref):
    tm = x_ref.shape[0]
    shape3 = (tm // SUBLANES, SUBLANES, D_MODEL)

    x = x_ref[...]
    xnb = _rms(x, g0_ref[...]).astype(BF16)

    g = (_proj(xnb, win_ref, 1) * _proj(xnb, win_ref, 2)).reshape(shape3)
    ya = (_shift_with_state(g, sa_ref, 2) * caw_ref[0:1, :]
          + _shift_with_state(g, sa_ref, 1) * caw_ref[1:2, :]
          + g * caw_ref[2:3, :]).reshape(tm, D_MODEL)
    na_ref[...] = g[:, SUBLANES - 2:, :]

    u = _proj(xnb, win_ref, 3).reshape(shape3)
    uc = (_shift_with_state(u, sb_ref, 3) * cbw_ref[0:1, :]
          + _shift_with_state(u, sb_ref, 2) * cbw_ref[1:2, :]
          + _shift_with_state(u, sb_ref, 1) * cbw_ref[2:3, :]
          + u * cbw_ref[3:4, :]) + cbb_ref[...]
    nb_ref[...] = u[:, SUBLANES - 3:, :]

    a, b = _lru_coeffs(uc.reshape(tm, D_MODEL), wga_ref, bga_ref, wgx_ref, bgx_ref, lam_ref)
    a, b = _group_scan(a.reshape(shape3), b.reshape(shape3))
    h = b + a * sh_ref[...]
    nh_ref[...] = h[:, SUBLANES - 1:, :]

    y_ref[...] = _mix_tail(x, xnb, ya, h.reshape(tm, D_MODEL), win_ref, g1_ref, wco_ref, wro_ref, wmo_ref)


def _mix_weight_args(w, l):
    row = _layer_spec((None, 1, D_MODEL), l)
    sq = _layer_spec((None, D_MODEL, D_MODEL), l)
    gate = _layer_spec((None, GATE_CHUNKS, MXU_DIM, MXU_DIM), l)
    args = [w["gains"], w["gains"], w["w_in"], w["conv_a_w"], w["w_conv_out"], w["conv_b_w"], w["conv_b_b"],
            w["w_gate_a"], w["b_gate_a"], w["w_gate_x"], w["b_gate_x"], w["lru_lambda"],
            w["w_rnn_out"], w["w_mix_out"]]
    specs = [_gain_spec(l, 0), _gain_spec(l, 1),
             _layer_spec((None, D_MODEL, N_PROJ * D_MODEL), l),
             _layer_spec((None, 3, D_MODEL), l), sq,
             _layer_spec((None, 4, D_MODEL), l), row,
             gate, row, gate, row, row, sq, sq]
    return args, specs


def _mix_prompt(w, l, x, bsz):
    n_tok = x.shape[0]
    nt = n_tok // bsz // TM_PROMPT
    wargs, wspecs = _mix_weight_args(w, l)
    tok_spec = pl.BlockSpec((TM_PROMPT, D_MODEL), lambda b, t: (b * nt + t, 0))

    def state_spec(rows):
        return pl.BlockSpec((None, rows, D_MODEL), lambda b, t: (b, 0, 0))

    return pl.pallas_call(
        _mix_prompt_kernel,
        grid=(bsz, nt),
        in_specs=[tok_spec] + wspecs,
        out_specs=[tok_spec, state_spec(2), state_spec(3), state_spec(1)],
        out_shape=[jax.ShapeDtypeStruct((n_tok, D_MODEL), F32),
                   jax.ShapeDtypeStruct((bsz, 2, D_MODEL), F32),
                   jax.ShapeDtypeStruct((bsz, 3, D_MODEL), F32),
                   jax.ShapeDtypeStruct((bsz, 1, D_MODEL), F32)],
        scratch_shapes=[pltpu.VMEM((TM_PROMPT + SUBLANES, D_MODEL), F32),
                        pltpu.VMEM((TM_PROMPT + SUBLANES, D_MODEL), F32),
                        pltpu.VMEM((SUBLANES, D_MODEL), F32)],
        compiler_params=_params(2),
        name="mix_prompt",
    )(x, *wargs)


def _mix_sample(w, l, x, state_a, state_b, state_h):
    n_tok = x.shape[0]
    n_seq = n_tok // SUBLANES
    wargs, wspecs = _mix_weight_args(w, l)
    tok_spec = pl.BlockSpec((SB_MIX * SUBLANES, D_MODEL), lambda i: (i, 0))

    def state_in(rows):
        return pl.BlockSpec((None, SB_MIX, rows, D_MODEL), lambda i: (l, i, 0, 0))

    def state_out(rows):
        return pl.BlockSpec((SB_MIX, rows, D_MODEL), lambda i: (i, 0, 0))

    return pl.pallas_call(
        _mix_sample_kernel,
        grid=(n_seq // SB_MIX,),
        in_specs=[tok_spec, state_in(2), state_in(3), state_in(1)] + wspecs,
        out_specs=[tok_spec, state_out(2), state_out(3), state_out(1)],
        out_shape=[jax.ShapeDtypeStruct((n_tok, D_MODEL), F32),
                   jax.ShapeDtypeStruct((n_seq, 2, D_MODEL), F32),
                   jax.ShapeDtypeStruct((n_seq, 3, D_MODEL), F32),
                   jax.ShapeDtypeStruct((n_seq, 1, D_MODEL), F32)],
        compiler_params=_params(1),
        name="mix_sample",
    )(x, state_a, state_b, state_h, *wargs)


def _softmax_rows(s):
    e = jnp.exp(s - jnp.max(s, axis=-1, keepdims=True))
    return e / jnp.sum(e, axis=-1, keepdims=True)


def _attn_prompt_kernel(x_ref, k_ref, v_ref, g2_ref, g3_ref, wq_ref, wo_ref, y_ref):
    x = x_ref[...]
    q = _bdot(_rms(x, g2_ref[...]), wq_ref[...]).astype(BF16)
    heads = []
    for h in range(MEM_HEADS):
        sl = slice(h * MEM_HEAD_DIM, (h + 1) * MEM_HEAD_DIM)
        kh = k_ref[:, sl].astype(BF16)
        vh = v_ref[:, sl].astype(BF16)
        s = lax.dot_general(q[:, sl], kh, (((1,), (1,)), ((), ())), preferred_element_type=F32)
        p = _softmax_rows(s * MEM_HEAD_DIM ** -0.5)
        heads.append(jnp.dot(p.astype(BF16), vh, preferred_element_type=F32))
    o = jnp.concatenate(heads, axis=-1)
    y_ref[...] = x + _rms(_bdot(o, wo_ref[...]), g3_ref[...])


def _attn_prompt(w, l, x, mem_k, mem_v, bsz):
    n_tok = x.shape[0]
    nt = n_tok // bsz // TM_PROMPT
    sq = _layer_spec((None, D_MODEL, D_MODEL), l)
    tok_spec = pl.BlockSpec((TM_PROMPT, D_MODEL), lambda b, t: (b * nt + t, 0))
    mem_spec = pl.BlockSpec((None, N_MEM, D_MODEL), lambda b, t: (l, b, 0))
    return pl.pallas_call(
        _attn_prompt_kernel,
        grid=(bsz, nt),
        in_specs=[tok_spec, mem_spec, mem_spec, _gain_spec(l, 2), _gain_spec(l, 3), sq, sq],
        out_specs=tok_spec,
        out_shape=jax.ShapeDtypeStruct((n_tok, D_MODEL), F32),
        compiler_params=_params(2),
        name="attn_prompt",
    )(x, mem_k, mem_v, w["gains"], w["gains"], w["w_q"], w["w_o"])


def _attn_sample_kernel(x_ref, k_ref, v_ref, g2_ref, g3_ref, wq_ref, wo_ref, y_ref, q_scr, o_scr):
    i = pl.program_id(0)
    sb = k_ref.shape[0]
    rows = sb * SUBLANES

    @pl.when(i == 0)
    def _():
        q_scr[...] = _bdot(_rms(x_ref[...], g2_ref[...]), wq_ref[...]).astype(BF16)

    start = pl.multiple_of(i * rows, rows)
    q = q_scr[pl.ds(start, rows), :]
    for h in range(MEM_HEADS):
        sl = slice(h * MEM_HEAD_DIM, (h + 1) * MEM_HEAD_DIM)
        qh = q[:, sl].reshape(sb, SUBLANES, MEM_HEAD_DIM)
        kh = k_ref[:, :, sl].astype(BF16)
        vh = v_ref[:, :, sl].astype(BF16)
        s = jnp.einsum("bqd,bkd->bqk", qh, kh, preferred_element_type=F32)
        p = _softmax_rows(s * MEM_HEAD_DIM ** -0.5)
        oh = jnp.einsum("bqk,bkd->bqd", p.astype(BF16), vh, preferred_element_type=F32)
        o_scr[pl.ds(start, rows), sl] = oh.reshape(rows, MEM_HEAD_DIM)

    @pl.when(i == pl.num_programs(0) - 1)
    def _():
        y_ref[...] = x_ref[...] + _rms(_bdot(o_scr[...], wo_ref[...]), g3_ref[...])


def _attn_sample(w, l, x, cache_k, cache_v):
    n_tok = x.shape[0]
    n_seq = n_tok // SUBLANES
    sq = _layer_spec((None, D_MODEL, D_MODEL), l)
    all_spec = pl.BlockSpec((n_tok, D_MODEL), lambda i: (0, 0), pipeline_mode=pl.Buffered(1))
    mem_spec = pl.BlockSpec((None, SB_ATTN, N_MEM, D_MODEL), lambda i: (l, i, 0, 0))
    return pl.pallas_call(
        _attn_sample_kernel,
        grid=(n_seq // SB_ATTN,),
        in_specs=[all_spec, mem_spec, mem_spec, _gain_spec(l, 2), _gain_spec(l, 3), sq, sq],
        out_specs=pl.BlockSpec((n_tok, D_MODEL), lambda i: (0, 0)),
        out_shape=jax.ShapeDtypeStruct((n_tok, D_MODEL), F32),
        scratch_shapes=[pltpu.VMEM((n_tok, D_MODEL), BF16), pltpu.VMEM((n_tok, D_MODEL), F32)],
        compiler_params=_params(1),
        name="attn_sample",
    )(x, cache_k, cache_v, w["gains"], w["gains"], w["w_q"], w["w_o"])


def _ffn_kernel(x_ref, g4_ref, g5_ref, wi_ref, wo_ref, y_ref):
    x = x_ref[...]
    xnb = _rms(x, g4_ref[...]).astype(BF16)
    gate = jnp.dot(xnb, wi_ref[:, :D_FF], preferred_element_type=F32)
    up = jnp.dot(xnb, wi_ref[:, D_FF:], preferred_element_type=F32)
    y = _bdot(jax.nn.silu(gate) * up, wo_ref[...])
    y_ref[...] = x + _rms(y, g5_ref[...])


def _ffn(w, l, x):
    n_tok = x.shape[0]
    tok_spec = pl.BlockSpec((TM_FFN, D_MODEL), lambda i: (i, 0))
    return pl.pallas_call(
        _ffn_kernel,
        grid=(n_tok // TM_FFN,),
        in_specs=[tok_spec, _gain_spec(l, 4), _gain_spec(l, 5),
                  _layer_spec((None, D_MODEL, 2 * D_FF), l), _layer_spec((None, D_FF, D_MODEL), l)],
        out_specs=tok_spec,
        out_shape=jax.ShapeDtypeStruct((n_tok, D_MODEL), F32),
        compiler_params=_params(1),
        name="ffn",
    )(x, w["gains"], w["gains"], w["w_ffn_in"], w["w_ffn_out"])


def _mem_kv_kernel(m_ref, g6_ref, wkv_ref, k_ref, v_ref):
    kv = _bdot(_rms(m_ref[...], g6_ref[...]), wkv_ref[...])
    k_ref[...] = kv[:, :D_MODEL]
    v_ref[...] = kv[:, D_MODEL:]


def _mem_kv(w, mem, depth):
    n_rows = mem.shape[0]
    out_spec = pl.BlockSpec((None, TM_MEM, D_MODEL), lambda l, i: (l, i, 0))
    out_shape = jax.ShapeDtypeStruct((depth, n_rows, D_MODEL), F32)
    return pl.pallas_call(
        _mem_kv_kernel,
        grid=(depth, n_rows // TM_MEM),
        in_specs=[pl.BlockSpec((TM_MEM, D_MODEL), lambda l, i: (i, 0)),
                  pl.BlockSpec((None, None, 1, D_MODEL), lambda l, i: (l, 6, 0, 0)),
                  pl.BlockSpec((None, D_MODEL, 2 * D_MODEL), lambda l, i: (l, 0, 0))],
        out_specs=[out_spec, out_spec],
        out_shape=[out_shape, out_shape],
        compiler_params=_params(2),
        name="mem_kv",
    )(mem, w["gains"], w["w_kv"])


def _block_diag_gate(wg):
    depth = wg.shape[0]
    w5 = wg.reshape(depth, GATE_CHUNKS, BLK_PER_CHUNK, BLK, BLK)
    eye = jnp.eye(BLK_PER_CHUNK, dtype=wg.dtype)
    dense = jnp.einsum("lcikm,ij->lcikjm", w5, eye)
    return dense.reshape(depth, GATE_CHUNKS, MXU_DIM, MXU_DIM).astype(BF16)


def kernel(x_prompt, x_sample, state_conv_a, state_conv_b, state_rglru, cache_mem_k, cache_mem_v, mem_prompt,
           norm_gains, w_in, conv_a_w, w_conv_out, conv_b_w, conv_b_b, w_gate_a, b_gate_a, w_gate_x, b_gate_x,
           lru_lambda, w_rnn_out, w_mix_out, w_kv_x, w_q_x, w_o_x, w_ffn_in, w_ffn_out):
    depth = norm_gains.shape[0]
    bsz, seq, _ = x_prompt.shape
    dec_bsz, dec_seq, _ = x_sample.shape
    assert dec_seq == SUBLANES and seq % TM_PROMPT == 0 and dec_bsz % SB_MIX == 0

    def row(v):
        return v.reshape(depth, 1, D_MODEL)

    w = dict(
        gains=norm_gains.reshape(depth, norm_gains.shape[1], 1, D_MODEL),
        w_in=w_in.astype(BF16), conv_a_w=conv_a_w, w_conv_out=w_conv_out.astype(BF16),
        conv_b_w=conv_b_w, conv_b_b=row(conv_b_b),
        w_gate_a=_block_diag_gate(w_gate_a), b_gate_a=row(b_gate_a),
        w_gate_x=_block_diag_gate(w_gate_x), b_gate_x=row(b_gate_x),
        lru_lambda=row(lru_lambda), w_rnn_out=w_rnn_out.astype(BF16), w_mix_out=w_mix_out.astype(BF16),
        w_kv=w_kv_x.astype(BF16), w_q=w_q_x.astype(BF16), w_o=w_o_x.astype(BF16),
        w_ffn_in=w_ffn_in.astype(BF16), w_ffn_out=w_ffn_out.astype(BF16),
    )

    xp = x_prompt.reshape(bsz * seq, D_MODEL)
    xs = x_sample.reshape(dec_bsz * dec_seq, D_MODEL)
    state_h = state_rglru.reshape(depth, dec_bsz, 1, D_MODEL)
    cache_k = cache_mem_k.reshape(depth, dec_bsz, N_MEM, D_MODEL)
    cache_v = cache_mem_v.reshape(depth, dec_bsz, N_MEM, D_MODEL)

    mem_k, mem_v = _mem_kv(w, mem_prompt.reshape(bsz * N_MEM, D_MODEL), depth)

    pa, pb, ph, sa, sb, sh = [], [], [], [], [], []
    for l in range(depth):
        xp, na, nb, nh = _mix_prompt(w, l, xp, bsz)
        pa.append(na); pb.append(nb); ph.append(nh)
        xp = _attn_prompt(w, l, xp, mem_k, mem_v, bsz)
        xp = _ffn(w, l, xp)

        xs, na, nb, nh = _mix_sample(w, l, xs, state_conv_a, state_conv_b, state_h)
        sa.append(na); sb.append(nb); sh.append(nh)
        xs = _attn_sample(w, l, xs, cache_k, cache_v)
        xs = _ffn(w, l, xs)

    mem_shape = (depth, bsz, N_MEM, MEM_HEADS, MEM_HEAD_DIM)
    return (xp.reshape(bsz, seq, D_MODEL), xs.reshape(dec_bsz, dec_seq, D_MODEL),
            jnp.stack(pa), jnp.stack(pb), jnp.stack(ph).reshape(depth, bsz, D_MODEL),
            mem_k.reshape(mem_shape), mem_v.reshape(mem_shape),
            jnp.stack(sa), jnp.stack(sb), jnp.stack(sh).reshape(depth, dec_bsz, D_MODEL))
```

```python
import math

import jax
import jax.numpy as jnp
from jax import lax
from jax.experimental import pallas as pl
from jax.experimental.pallas import tpu as pltpu

D_MODEL = 1024
N_MEM = 256
MEM_HEADS = 4
MEM_HEAD_DIM = D_MODEL // MEM_HEADS
N_BLK = 16
BLK = D_MODEL // N_BLK
LRU_C = 8.0
EPS = 1e-6
D_FF = int(math.ceil(8 * D_MODEL / 3 / 256) * 256)
N_PROJ = 6

SUBLANES = 8
MXU_DIM = 256
GATE_CHUNKS = D_MODEL // MXU_DIM
BLK_PER_CHUNK = MXU_DIM // BLK
VMEM_LIMIT = 56 * 1024 * 1024

TM_PROMPT = 512
SB_MIX = 64
SB_ATTN = 4
TM_FFN = 512
TM_MEM = 512

BF16 = jnp.bfloat16
F32 = jnp.float32


def _rms(x, g):
    ms = jnp.mean(x * x, axis=-1, keepdims=True)
    return x * lax.rsqrt(ms + EPS) * g


def _bdot(a, w):
    return jnp.dot(a.astype(BF16), w, preferred_element_type=F32)


def _layer_spec(block, layer, tail=None):
    idx = (layer,) + (tail if tail is not None else (0,) * (len(block) - 1))
    return pl.BlockSpec(block, lambda *_: idx, pipeline_mode=pl.Buffered(1))


def _gain_spec(layer, k):
    return _layer_spec((None, None, 1, D_MODEL), layer, (k, 0, 0))


def _params(n_axes):
    return pltpu.CompilerParams(dimension_semantics=("arbitrary",) * n_axes,
                                vmem_limit_bytes=VMEM_LIMIT)


def _group_scan(a, b):
    t = lax.broadcasted_iota(jnp.int32, (1, SUBLANES, a.shape[-1]), 1)
    d = 1
    while d < SUBLANES:
        keep = t >= d
        a_sh = jnp.where(keep, pltpu.roll(a, d, axis=1), 1.0)
        b_sh = jnp.where(keep, pltpu.roll(b, d, axis=1), 0.0)
        b = a * b_sh + b
        a = a * a_sh
        d *= 2
    return a, b


def _lru_coeffs(uc, wga_ref, bga_ref, wgx_ref, bgx_ref, lam_ref):
    ucb = uc.astype(BF16)
    r_parts, i_parts = [], []
    for c in range(GATE_CHUNKS):
        sl = slice(c * MXU_DIM, (c + 1) * MXU_DIM)
        r_parts.append(jnp.dot(ucb[:, sl], wga_ref[c], preferred_element_type=F32))
        i_parts.append(jnp.dot(ucb[:, sl], wgx_ref[c], preferred_element_type=F32))
    r = jax.nn.sigmoid(jnp.concatenate(r_parts, axis=-1) + bga_ref[...])
    i = jax.nn.sigmoid(jnp.concatenate(i_parts, axis=-1) + bgx_ref[...])
    lam = lam_ref[...]
    softplus_neg_lam = jnp.maximum(-lam, 0.0) + jnp.log1p(jnp.exp(-jnp.abs(lam)))
    log_a = (-LRU_C * r) * softplus_neg_lam
    a = jnp.exp(log_a)
    mult = jnp.sqrt(-jnp.tanh(log_a) * (1.0 + a * a))
    return a, mult * (i * uc)


def _proj(xnb, win_ref, k):
    return jnp.dot(xnb, win_ref[:, k * D_MODEL:(k + 1) * D_MODEL], preferred_element_type=F32)


def _mix_tail(x, xnb, ya, hseq, win_ref, g1_ref, wco_ref, wro_ref, wmo_ref):
    y_conv = _bdot(_proj(xnb, win_ref, 0) * ya, wco_ref[...])
    y_rnn = _bdot(hseq, wro_ref[...])
    z = jax.nn.sigmoid(_proj(xnb, win_ref, 4)) * y_conv + jax.nn.sigmoid(_proj(xnb, win_ref, 5)) * y_rnn
    return x + _rms(_bdot(z, wmo_ref[...]), g1_ref[...])


def _mix_prompt_kernel(x_ref, g0_ref, g1_ref, win_ref, caw_ref, wco_ref, cbw_ref, cbb_ref,
                       wga_ref, bga_ref, wgx_ref, bgx_ref, lam_ref, wro_ref, wmo_ref,
                       y_ref, na_ref, nb_ref, nh_ref,
                       ga_scr, ub_scr, h_scr):
    t_idx = pl.program_id(1)
    tm = x_ref.shape[0]
    groups = tm // SUBLANES

    @pl.when(t_idx == 0)
    def _():
        ga_scr[0:SUBLANES, :] = jnp.zeros((SUBLANES, D_MODEL), F32)
        ub_scr[0:SUBLANES, :] = jnp.zeros((SUBLANES, D_MODEL), F32)
        h_scr[...] = jnp.zeros_like(h_scr)

    x = x_ref[...]
    xnb = _rms(x, g0_ref[...]).astype(BF16)

    g = _proj(xnb, win_ref, 1) * _proj(xnb, win_ref, 2)
    ga_scr[SUBLANES:, :] = g
    ya = (ga_scr[pl.ds(SUBLANES - 2, tm), :] * caw_ref[0:1, :]
          + ga_scr[pl.ds(SUBLANES - 1, tm), :] * caw_ref[1:2, :]
          + g * caw_ref[2:3, :])
    ga_scr[0:SUBLANES, :] = ga_scr[pl.ds(tm, SUBLANES), :]

    u = _proj(xnb, win_ref, 3)
    ub_scr[SUBLANES:, :] = u
    uc = (ub_scr[pl.ds(SUBLANES - 3, tm), :] * cbw_ref[0:1, :]
          + ub_scr[pl.ds(SUBLANES - 2, tm), :] * cbw_ref[1:2, :]
          + ub_scr[pl.ds(SUBLANES - 1, tm), :] * cbw_ref[2:3, :]
          + u * cbw_ref[3:4, :]) + cbb_ref[...]
    ub_scr[0:SUBLANES, :] = ub_scr[pl.ds(tm, SUBLANES), :]

    a, b = _lru_coeffs(uc, wga_ref, bga_ref, wgx_ref, bgx_ref, lam_ref)
    a, b = _group_scan(a.reshape(groups, SUBLANES, D_MODEL), b.reshape(groups, SUBLANES, D_MODEL))
    carry = h_scr[...]
    h_groups = []
    for j in range(groups):
        hj = b[j] + a[j] * carry
        h_groups.append(hj)
        carry = jnp.broadcast_to(hj[SUBLANES - 1:SUBLANES, :], (SUBLANES, D_MODEL))
    h_scr[...] = carry
    hseq = jnp.concatenate(h_groups, axis=0)

    y_ref[...] = _mix_tail(x, xnb, ya, hseq, win_ref, g1_ref, wco_ref, wro_ref, wmo_ref)

    @pl.when(t_idx == pl.num_programs(1) - 1)
    def _():
        na_ref[...] = ga_scr[pl.ds(SUBLANES - 2, 2), :]
        nb_ref[...] = ub_scr[pl.ds(SUBLANES - 3, 3), :]
        nh_ref[...] = h_scr[0:1, :]


def _shift_with_state(v, st_ref, s):
    n_state = st_ref.shape[1]
    t = lax.broadcasted_iota(jnp.int32, (1, SUBLANES, v.shape[-1]), 1)
    out = pltpu.roll(v, s, axis=1)
    for k in range(s):
        row = n_state - s + k
        out = jnp.where(t == k, st_ref[:, row:row + 1, :], out)
    return out


def _mix_sample_kernel(x_ref, sa_ref, sb_ref, sh_ref, g0_ref, g1_ref, win_ref, caw_ref, wco_ref,
                       cbw_ref, cbb_ref, wga_ref, bga_ref, wgx_ref, bgx_ref, lam_ref, wro_ref, wmo_ref,
                       y_ref, na_ref, nb_ref, nh_ref):
    tm = x_ref.shape[0]
    shape3 = (tm // SUBLANES, SUBLANES, D_MODEL)

    x = x_ref[...]
    xnb = _rms(x, g0_ref[...]).astype(BF16)

    g = (_proj(xnb, win_ref, 1) * _proj(xnb, win_ref, 2)).reshape(shape3)
    ya = (_shift_with_state(g, sa_ref, 2) * caw_ref[0:1, :]
          + _shift_with_state(g, sa_ref, 1) * caw_ref[1:2, :]
          + g * caw_ref[2:3, :]).reshape(tm, D_MODEL)
    na_ref[...] = g[:, SUBLANES - 2:, :]

    u = _proj(xnb, win_ref, 3).reshape(shape3)
    uc = (_shift_with_state(u, sb_ref, 3) * cbw_ref[0:1, :]
          + _shift_with_state(u, sb_ref, 2) * cbw_ref[1:2, :]
          + _shift_with_state(u, sb_ref, 1) * cbw_ref[2:3, :]
          + u * cbw_ref[3:4, :]) + cbb_ref[...]
    nb_ref[...] = u[:, SUBLANES - 3:, :]

    a, b = _lru_coeffs(uc.reshape(tm, D_MODEL), wga_ref, bga_ref, wgx_ref, bgx_ref, lam_ref)
    a, b = _group_scan(a.reshape(shape3), b.reshape(shape3))
    h = b + a * sh_ref[...]
    nh_ref[...] = h[:, SUBLANES - 1:, :]

    y_ref[...] = _mix_tail(x, xnb, ya, h.reshape(tm, D_MODEL), win_ref, g1_ref, wco_ref, wro_ref, wmo_ref)


def _mix_weight_args(w, l):
    row = _layer_spec((None, 1, D_MODEL), l)
    sq = _layer_spec((None, D_MODEL, D_MODEL), l)
    gate = _layer_spec((None, GATE_CHUNKS, MXU_DIM, MXU_DIM), l)
    args = [w["gains"], w["gains"], w["w_in"], w["conv_a_w"], w["w_conv_out"], w["conv_b_w"], w["conv_b_b"],
            w["w_gate_a"], w["b_gate_a"], w["w_gate_x"], w["b_gate_x"], w["lru_lambda"],
            w["w_rnn_out"], w["w_mix_out"]]
    specs = [_gain_spec(l, 0), _gain_spec(l, 1),
             _layer_spec((None, D_MODEL, N_PROJ * D_MODEL), l),
             _layer_spec((None, 3, D_MODEL), l), sq,
             _layer_spec((None, 4, D_MODEL), l), row,
             gate, row, gate, row, row, sq, sq]
    return args, specs


def _mix_prompt(w, l, x, bsz):
    n_tok = x.shape[0]
    nt = n_tok // bsz // TM_PROMPT
    wargs, wspecs = _mix_weight_args(w, l)
    tok_spec = pl.BlockSpec((TM_PROMPT, D_MODEL), lambda b, t: (b * nt + t, 0))

    def state_spec(rows):
        return pl.BlockSpec((None, rows, D_MODEL), lambda b, t: (b, 0, 0))

    return pl.pallas_call(
        _mix_prompt_kernel,
        grid=(bsz, nt),
        in_specs=[tok_spec] + wspecs,
        out_specs=[tok_spec, state_spec(2), state_spec(3), state_spec(1)],
        out_shape=[jax.ShapeDtypeStruct((n_tok, D_MODEL), F32),
                   jax.ShapeDtypeStruct((bsz, 2, D_MODEL), F32),
                   jax.ShapeDtypeStruct((bsz, 3, D_MODEL), F32),
                   jax.ShapeDtypeStruct((bsz, 1, D_MODEL), F32)],
        scratch_shapes=[pltpu.VMEM((TM_PROMPT + SUBLANES, D_MODEL), F32),
                        pltpu.VMEM((TM_PROMPT + SUBLANES, D_MODEL), F32),
                        pltpu.VMEM((SUBLANES, D_MODEL), F32)],
        compiler_params=_params(2),
        name="mix_prompt",
    )(x, *wargs)


def _mix_sample(w, l, x, state_a, state_b, state_h):
    n_tok = x.shape[0]
    n_seq = n_tok // SUBLANES
    wargs, wspecs = _mix_weight_args(w, l)
    tok_spec = pl.BlockSpec((SB_MIX * SUBLANES, D_MODEL), lambda i: (i, 0))

    def state_in(rows):
        return pl.BlockSpec((None, SB_MIX, rows, D_MODEL), lambda i: (l, i, 0, 0))

    def state_out(rows):
        return pl.BlockSpec((SB_MIX, rows, D_MODEL), lambda i: (i, 0, 0))

    return pl.pallas_call(
        _mix_sample_kernel,
        grid=(n_seq // SB_MIX,),
        in_specs=[tok_spec, state_in(2), state_in(3), state_in(1)] + wspecs,
        out_specs=[tok_spec, state_out(2), state_out(3), state_out(1)],
        out_shape=[jax.ShapeDtypeStruct((n_tok, D_MODEL), F32),
                   jax.ShapeDtypeStruct((n_seq, 2, D_MODEL), F32),
                   jax.ShapeDtypeStruct((n_seq, 3, D_MODEL), F32),
                   jax.ShapeDtypeStruct((n_seq, 1, D_MODEL), F32)],
        compiler_params=_params(1),
        name="mix_sample",
    )(x, state_a, state_b, state_h, *wargs)


def _softmax_rows(s):
    e = jnp.exp(s - jnp.max(s, axis=-1, keepdims=True))
    return e / jnp.sum(e, axis=-1, keepdims=True)


def _attn_prompt_kernel(x_ref, k_ref, v_ref, g2_ref, g3_ref, wq_ref, wo_ref, y_ref):
    x = x_ref[...]
    q = _bdot(_rms(x, g2_ref[...]), wq_ref[...]).astype(BF16)
    heads = []
    for h in range(MEM_HEADS):
        sl = slice(h * MEM_HEAD_DIM, (h + 1) * MEM_HEAD_DIM)
        kh = k_ref[:, sl].astype(BF16)
        vh = v_ref[:, sl].astype(BF16)
        s = lax.dot_general(q[:, sl], kh, (((1,), (1,)), ((), ())), preferred_element_type=F32)
        p = _softmax_rows(s * MEM_HEAD_DIM ** -0.5)
        heads.append(jnp.dot(p.astype(BF16), vh, preferred_element_type=F32))
    o = jnp.concatenate(heads, axis=-1)
    y_ref[...] = x + _rms(_bdot(o, wo_ref[...]), g3_ref[...])


def _attn_prompt(w, l, x, mem_k, mem_v, bsz):
    n_tok = x.shape[0]
    nt = n_tok // bsz // TM_PROMPT
    sq = _layer_spec((None, D_MODEL, D_MODEL), l)
    tok_spec = pl.BlockSpec((TM_PROMPT, D_MODEL), lambda b, t: (b * nt + t, 0))
    mem_spec = pl.BlockSpec((None, N_MEM, D_MODEL), lambda b, t: (l, b, 0))
    return pl.pallas_call(
        _attn_prompt_kernel,
        grid=(bsz, nt),
        in_specs=[tok_spec, mem_spec, mem_spec, _gain_spec(l, 2), _gain_spec(l, 3), sq, sq],
        out_specs=tok_spec,
        out_shape=jax.ShapeDtypeStruct((n_tok, D_MODEL), F32),
        compiler_params=_params(2),
        name="attn_prompt",
    )(x, mem_k, mem_v, w["gains"], w["gains"], w["w_q"], w["w_o"])


def _attn_sample_kernel(x_ref, k_ref, v_ref, g2_ref, g3_ref, wq_ref, wo_ref, y_ref, q_scr, o_scr):
    i = pl.program_id(0)
    sb = k_ref.shape[0]
    rows = sb * SUBLANES

    @pl.when(i == 0)
    def _():
        q_scr[...] = _bdot(_rms(x_ref[...], g2_ref[...]), wq_ref[...]).astype(BF16)

    hq = MEM_HEADS * SUBLANES
    row_head = lax.broadcasted_iota(jnp.int32, (hq, N_MEM * MEM_HEADS), 0) // SUBLANES
    col_head = lax.broadcasted_iota(jnp.int32, (hq, N_MEM * MEM_HEADS), 1) % MEM_HEADS
    same_head = row_head == col_head
    start = pl.multiple_of(i * rows, rows)
    q = q_scr[pl.ds(start, rows), :]
    for b in range(sb):
        qb = q[b * SUBLANES:(b + 1) * SUBLANES, :]
        qs = jnp.concatenate([qb[:, h * MEM_HEAD_DIM:(h + 1) * MEM_HEAD_DIM] for h in range(MEM_HEADS)], axis=0)
        k2 = k_ref[b].reshape(N_MEM * MEM_HEADS, MEM_HEAD_DIM).astype(BF16)
        v2 = v_ref[b].reshape(N_MEM * MEM_HEADS, MEM_HEAD_DIM).astype(BF16)
        s = lax.dot_general(qs, k2, (((1,), (1,)), ((), ())), preferred_element_type=F32)
        p = _softmax_rows(jnp.where(same_head, s * MEM_HEAD_DIM ** -0.5, -jnp.inf))
        o = jnp.dot(p.astype(BF16), v2, preferred_element_type=F32)
        o_scr[pl.ds(start + b * SUBLANES, SUBLANES), :] = jnp.concatenate(
            [o[h * SUBLANES:(h + 1) * SUBLANES, :] for h in range(MEM_HEADS)], axis=1)

    @pl.when(i == pl.num_programs(0) - 1)
    def _():
        y_ref[...] = x_ref[...] + _rms(_bdot(o_scr[...], wo_ref[...]), g3_ref[...])


def _attn_sample(w, l, x, cache_k, cache_v):
    n_tok = x.shape[0]
    n_seq = n_tok // SUBLANES
    sq = _layer_spec((None, D_MODEL, D_MODEL), l)
    all_spec = pl.BlockSpec((n_tok, D_MODEL), lambda i: (0, 0), pipeline_mode=pl.Buffered(1))
    mem_spec = pl.BlockSpec((None, SB_ATTN, N_MEM, MEM_HEADS, MEM_HEAD_DIM), lambda i: (l, i, 0, 0, 0))
    return pl.pallas_call(
        _attn_sample_kernel,
        grid=(n_seq // SB_ATTN,),
        in_specs=[all_spec, mem_spec, mem_spec, _gain_spec(l, 2), _gain_spec(l, 3), sq, sq],
        out_specs=pl.BlockSpec((n_tok, D_MODEL), lambda i: (0, 0)),
        out_shape=jax.ShapeDtypeStruct((n_tok, D_MODEL), F32),
        scratch_shapes=[pltpu.VMEM((n_tok, D_MODEL), BF16), pltpu.VMEM((n_tok, D_MODEL), F32)],
        compiler_params=_params(1),
        name="attn_sample",
    )(x, cache_k, cache_v, w["gains"], w["gains"], w["w_q"], w["w_o"])


def _ffn_kernel(x_ref, g4_ref, g5_ref, wi_ref, wo_ref, y_ref):
    x = x_ref[...]
    xnb = _rms(x, g4_ref[...]).astype(BF16)
    gate = jnp.dot(xnb, wi_ref[:, :D_FF], preferred_element_type=F32)
    up = jnp.dot(xnb, wi_ref[:, D_FF:], preferred_element_type=F32)
    y = _bdot(jax.nn.silu(gate) * up, wo_ref[...])
    y_ref[...] = x + _rms(y, g5_ref[...])


def _ffn(w, l, x):
    n_tok = x.shape[0]
    tok_spec = pl.BlockSpec((TM_FFN, D_MODEL), lambda i: (i, 0))
    return pl.pallas_call(
        _ffn_kernel,
        grid=(n_tok // TM_FFN,),
        in_specs=[tok_spec, _gain_spec(l, 4), _gain_spec(l, 5),
                  _layer_spec((None, D_MODEL, 2 * D_FF), l), _layer_spec((None, D_FF, D_MODEL), l)],
        out_specs=tok_spec,
        out_shape=jax.ShapeDtypeStruct((n_tok, D_MODEL), F32),
        compiler_params=_params(1),
        name="ffn",
    )(x, w["gains"], w["gains"], w["w_ffn_in"], w["w_ffn_out"])


def _mem_kv_kernel(m_ref, g6_ref, wkv_ref, k_ref, v_ref):
    kv = _bdot(_rms(m_ref[...], g6_ref[...]), wkv_ref[...])
    k_ref[...] = kv[:, :D_MODEL]
    v_ref[...] = kv[:, D_MODEL:]


def _mem_kv(w, mem, depth):
    n_rows = mem.shape[0]
    out_spec = pl.BlockSpec((None, TM_MEM, D_MODEL), lambda l, i: (l, i, 0))
    out_shape = jax.ShapeDtypeStruct((depth, n_rows, D_MODEL), F32)
    return pl.pallas_call(
        _mem_kv_kernel,
        grid=(depth, n_rows // TM_MEM),
        in_specs=[pl.BlockSpec((TM_MEM, D_MODEL), lambda l, i: (i, 0)),
                  pl.BlockSpec((None, None, 1, D_MODEL), lambda l, i: (l, 6, 0, 0)),
                  pl.BlockSpec((None, D_MODEL, 2 * D_MODEL), lambda l, i: (l, 0, 0))],
        out_specs=[out_spec, out_spec],
        out_shape=[out_shape, out_shape],
        compiler_params=_params(2),
        name="mem_kv",
    )(mem, w["gains"], w["w_kv"])


def _block_diag_gate(wg):
    depth = wg.shape[0]
    w5 = wg.reshape(depth, GATE_CHUNKS, BLK_PER_CHUNK, BLK, BLK)
    eye = jnp.eye(BLK_PER_CHUNK, dtype=wg.dtype)
    dense = jnp.einsum("lcikm,ij->lcikjm", w5, eye)
    return dense.reshape(depth, GATE_CHUNKS, MXU_DIM, MXU_DIM).astype(BF16)


def kernel(x_prompt, x_sample, state_conv_a, state_conv_b, state_rglru, cache_mem_k, cache_mem_v, mem_prompt,
           norm_gains, w_in, conv_a_w, w_conv_out, conv_b_w, conv_b_b, w_gate_a, b_gate_a, w_gate_x, b_gate_x,
           lru_lambda, w_rnn_out, w_mix_out, w_kv_x, w_q_x, w_o_x, w_ffn_in, w_ffn_out):
    depth = norm_gains.shape[0]
    bsz, seq, _ = x_prompt.shape
    dec_bsz, dec_seq, _ = x_sample.shape
    assert dec_seq == SUBLANES and seq % TM_PROMPT == 0 and dec_bsz % SB_MIX == 0

    def row(v):
        return v.reshape(depth, 1, D_MODEL)

    w = dict(
        gains=norm_gains.reshape(depth, norm_gains.shape[1], 1, D_MODEL),
        w_in=w_in.astype(BF16), conv_a_w=conv_a_w, w_conv_out=w_conv_out.astype(BF16),
        conv_b_w=conv_b_w, conv_b_b=row(conv_b_b),
        w_gate_a=_block_diag_gate(w_gate_a), b_gate_a=row(b_gate_a),
        w_gate_x=_block_diag_gate(w_gate_x), b_gate_x=row(b_gate_x),
        lru_lambda=row(lru_lambda), w_rnn_out=w_rnn_out.astype(BF16), w_mix_out=w_mix_out.astype(BF16),
        w_kv=w_kv_x.astype(BF16), w_q=w_q_x.astype(BF16), w_o=w_o_x.astype(BF16),
        w_ffn_in=w_ffn_in.astype(BF16), w_ffn_out=w_ffn_out.astype(BF16),
    )

    xp = x_prompt.reshape(bsz * seq, D_MODEL)
    xs = x_sample.reshape(dec_bsz * dec_seq, D_MODEL)
    state_h = state_rglru.reshape(depth, dec_bsz, 1, D_MODEL)
    mem_k, mem_v = _mem_kv(w, mem_prompt.reshape(bsz * N_MEM, D_MODEL), depth)

    pa, pb, ph, sa, sb, sh = [], [], [], [], [], []
    for l in range(depth):
        xp, na, nb, nh = _mix_prompt(w, l, xp, bsz)
        pa.append(na); pb.append(nb); ph.append(nh)
        xp = _attn_prompt(w, l, xp, mem_k, mem_v, bsz)
        xp = _ffn(w, l, xp)

        xs, na, nb, nh = _mix_sample(w, l, xs, state_conv_a, state_conv_b, state_h)
        sa.append(na); sb.append(nb); sh.append(nh)
        xs = _attn_sample(w, l, xs, cache_mem_k, cache_mem_v)
        xs = _ffn(w, l, xs)

    mem_shape = (depth, bsz, N_MEM, MEM_HEADS, MEM_HEAD_DIM)
    return (xp.reshape(bsz, seq, D_MODEL), xs.reshape(dec_bsz, dec_seq, D_MODEL),
            jnp.stack(pa), jnp.stack(pb), jnp.stack(ph).reshape(depth, bsz, D_MODEL),
            mem_k.reshape(mem_shape), mem_v.reshape(mem_shape),
            jnp.stack(sa), jnp.stack(sb), jnp.stack(sh).reshape(depth, dec_bsz, D_MODEL))
```

```python
import math

import jax
import jax.numpy as jnp
from jax import lax
from jax.experimental import pallas as pl
from jax.experimental.pallas import tpu as pltpu

D_MODEL = 1024
N_MEM = 256
MEM_HEADS = 4
MEM_HEAD_DIM = D_MODEL // MEM_HEADS
N_BLK = 16
BLK = D_MODEL // N_BLK
LRU_C = 8.0
EPS = 1e-6
D_FF = int(math.ceil(8 * D_MODEL / 3 / 256) * 256)
N_PROJ = 6

SUBLANES = 8
MXU_DIM = 256
GATE_CHUNKS = D_MODEL // MXU_DIM
BLK_PER_CHUNK = MXU_DIM // BLK
VMEM_LIMIT = 56 * 1024 * 1024

TM_PROMPT = 512
SB_MIX = 64
SB_ATTN = 4
TM_FFN = 512
TM_MEM = 512

BF16 = jnp.bfloat16
F32 = jnp.float32


def _rms(x, g):
    ms = jnp.mean(x * x, axis=-1, keepdims=True)
    return x * lax.rsqrt(ms + EPS) * g


def _bdot(a, w):
    return jnp.dot(a.astype(BF16), w, preferred_element_type=F32)


def _layer_spec(block, layer, tail=None):
    idx = (layer,) + (tail if tail is not None else (0,) * (len(block) - 1))
    return pl.BlockSpec(block, lambda *_: idx, pipeline_mode=pl.Buffered(1))


def _gain_spec(layer, k):
    return _layer_spec((None, None, 1, D_MODEL), layer, (k, 0, 0))


def _params(n_axes):
    return pltpu.CompilerParams(dimension_semantics=("arbitrary",) * n_axes,
                                vmem_limit_bytes=VMEM_LIMIT)


def _group_scan(a, b):
    t = lax.broadcasted_iota(jnp.int32, (1, SUBLANES, a.shape[-1]), 1)
    d = 1
    while d < SUBLANES:
        keep = t >= d
        a_sh = jnp.where(keep, pltpu.roll(a, d, axis=1), 1.0)
        b_sh = jnp.where(keep, pltpu.roll(b, d, axis=1), 0.0)
        b = a * b_sh + b
        a = a * a_sh
        d *= 2
    return a, b


def _gate_dots(uc, wga_ref, wgx_ref):
    ucb = uc.astype(BF16)
    r_parts, i_parts = [], []
    for c in range(GATE_CHUNKS):
        sl = slice(c * MXU_DIM, (c + 1) * MXU_DIM)
        r_parts.append(jnp.dot(ucb[:, sl], wga_ref[c], preferred_element_type=F32))
        i_parts.append(jnp.dot(ucb[:, sl], wgx_ref[c], preferred_element_type=F32))
    return jnp.concatenate(r_parts, axis=-1), jnp.concatenate(i_parts, axis=-1)


def _lru_coeffs(uc, r_pre, i_pre, bga_ref, bgx_ref, lam_ref):
    r = jax.nn.sigmoid(r_pre + bga_ref[...])
    i = jax.nn.sigmoid(i_pre + bgx_ref[...])
    lam = lam_ref[...]
    softplus_neg_lam = jnp.maximum(-lam, 0.0) + jnp.log1p(jnp.exp(-jnp.abs(lam)))
    log_a = (-LRU_C * r) * softplus_neg_lam
    a = jnp.exp(log_a)
    mult = jnp.sqrt(-jnp.tanh(log_a) * (1.0 + a * a))
    return a, mult * (i * uc)


def _proj(xnb, win_ref, k):
    return jnp.dot(xnb, win_ref[:, k * D_MODEL:(k + 1) * D_MODEL], preferred_element_type=F32)


def _mix_body(x_ref, w_refs, delayed_a, delayed_b, finish_scan):
    (g0_ref, g1_ref, win_ref, caw_ref, wco_ref, cbw_ref, cbb_ref,
     wga_ref, bga_ref, wgx_ref, bgx_ref, lam_ref, wro_ref, wmo_ref) = w_refs
    tm = x_ref.shape[0]
    shape3 = (tm // SUBLANES, SUBLANES, D_MODEL)
    x = x_ref[...]
    xnb = _rms(x, g0_ref[...]).astype(BF16)

    u = _proj(xnb, win_ref, 3)
    hc = _proj(xnb, win_ref, 1)
    u3, u2, u1 = delayed_b(u)
    uc = (u3 * cbw_ref[0:1, :] + u2 * cbw_ref[1:2, :] + u1 * cbw_ref[2:3, :] + u * cbw_ref[3:4, :]) + cbb_ref[...]
    r_pre, i_pre = _gate_dots(uc, wga_ref, wgx_ref)
    hh = _proj(xnb, win_ref, 2)
    a, b = _lru_coeffs(uc, r_pre, i_pre, bga_ref, bgx_ref, lam_ref)
    hb = _proj(xnb, win_ref, 0)
    gc = _proj(xnb, win_ref, 4)
    gr = _proj(xnb, win_ref, 5)
    a, b = _group_scan(a.reshape(shape3), b.reshape(shape3))
    hseq = finish_scan(a, b)

    g = hc * hh
    g2, g1 = delayed_a(g)
    ya = g2 * caw_ref[0:1, :] + g1 * caw_ref[1:2, :] + g * caw_ref[2:3, :]
    y_rnn = _bdot(hseq, wro_ref[...])
    y_conv = _bdot(hb * ya, wco_ref[...])
    z = jax.nn.sigmoid(gc) * y_conv + jax.nn.sigmoid(gr) * y_rnn
    return x + _rms(_bdot(z, wmo_ref[...]), g1_ref[...])


def _shift_rows(v, tail, s):
    t = lax.broadcasted_iota(jnp.int32, (SUBLANES, v.shape[-1]), 0)
    rolled = pltpu.roll(v, s, axis=0)
    first = jnp.where(t < s, pltpu.roll(tail, s, axis=0), rolled[0:SUBLANES, :])
    return jnp.concatenate([first, rolled[SUBLANES:, :]], axis=0)


def _mix_prompt_kernel(x_ref, *refs):
    w_refs = refs[:14]
    y_ref, na_ref, nb_ref, nh_ref, ga_scr, ub_scr, h_scr = refs[14:]
    t_idx = pl.program_id(1)
    tm = x_ref.shape[0]

    @pl.when(t_idx == 0)
    def _():
        ga_scr[...] = jnp.zeros_like(ga_scr)
        ub_scr[...] = jnp.zeros_like(ub_scr)
        h_scr[...] = jnp.zeros_like(h_scr)

    def delayed_a(g):
        tail = ga_scr[...]
        ga_scr[...] = g[tm - SUBLANES:, :]
        return _shift_rows(g, tail, 2), _shift_rows(g, tail, 1)

    def delayed_b(u):
        tail = ub_scr[...]
        ub_scr[...] = u[tm - SUBLANES:, :]
        return _shift_rows(u, tail, 3), _shift_rows(u, tail, 2), _shift_rows(u, tail, 1)

    def finish_scan(a, b):
        carry = h_scr[...]
        h_groups = []
        for j in range(a.shape[0]):
            hj = b[j] + a[j] * carry
            h_groups.append(hj)
            carry = jnp.broadcast_to(hj[SUBLANES - 1:SUBLANES, :], (SUBLANES, D_MODEL))
        h_scr[...] = carry
        return jnp.concatenate(h_groups, axis=0)

    y_ref[...] = _mix_body(x_ref, w_refs, delayed_a, delayed_b, finish_scan)

    @pl.when(t_idx == pl.num_programs(1) - 1)
    def _():
        na_ref[...] = ga_scr[SUBLANES - 2:, :]
        nb_ref[...] = ub_scr[SUBLANES - 3:, :]
        nh_ref[...] = h_scr[0:1, :]


def _shift_with_state(v, st_ref, s):
    n_state = st_ref.shape[1]
    t = lax.broadcasted_iota(jnp.int32, (1, SUBLANES, v.shape[-1]), 1)
    out = pltpu.roll(v, s, axis=1)
    for k in range(s):
        row = n_state - s + k
        out = jnp.where(t == k, st_ref[:, row:row + 1, :], out)
    return out


def _mix_sample_kernel(x_ref, sa_ref, sb_ref, sh_ref, *refs):
    w_refs = refs[:14]
    y_ref, na_ref, nb_ref, nh_ref = refs[14:]
    tm = x_ref.shape[0]
    shape3 = (tm // SUBLANES, SUBLANES, D_MODEL)

    def delayed(v, st_ref, new_ref, shifts):
        v3 = v.reshape(shape3)
        new_ref[...] = v3[:, SUBLANES - st_ref.shape[1]:, :]
        return tuple(_shift_with_state(v3, st_ref, s).reshape(tm, D_MODEL) for s in shifts)

    def finish_scan(a, b):
        h = b + a * sh_ref[...]
        nh_ref[...] = h[:, SUBLANES - 1:, :]
        return h.reshape(tm, D_MODEL)

    y_ref[...] = _mix_body(x_ref, w_refs,
                           lambda g: delayed(g, sa_ref, na_ref, (2, 1)),
                           lambda u: delayed(u, sb_ref, nb_ref, (3, 2, 1)),
                           finish_scan)


def _mix_weight_args(w, l):
    row = _layer_spec((None, 1, D_MODEL), l)
    sq = _layer_spec((None, D_MODEL, D_MODEL), l)
    gate = _layer_spec((None, GATE_CHUNKS, MXU_DIM, MXU_DIM), l)
    args = [w["gains"], w["gains"], w["w_in"], w["conv_a_w"], w["w_conv_out"], w["conv_b_w"], w["conv_b_b"],
            w["w_gate_a"], w["b_gate_a"], w["w_gate_x"], w["b_gate_x"], w["lru_lambda"],
            w["w_rnn_out"], w["w_mix_out"]]
    specs = [_gain_spec(l, 0), _gain_spec(l, 1),
             _layer_spec((None, D_MODEL, N_PROJ * D_MODEL), l),
             _layer_spec((None, 3, D_MODEL), l), sq,
             _layer_spec((None, 4, D_MODEL), l), row,
             gate, row, gate, row, row, sq, sq]
    return args, specs


def _mix_prompt(w, l, x, bsz):
    n_tok = x.shape[0]
    nt = n_tok // bsz // TM_PROMPT
    wargs, wspecs = _mix_weight_args(w, l)
    tok_spec = pl.BlockSpec((TM_PROMPT, D_MODEL), lambda b, t: (b * nt + t, 0))

    def state_spec(rows):
        return pl.BlockSpec((None, rows, D_MODEL), lambda b, t: (b, 0, 0))

    return pl.pallas_call(
        _mix_prompt_kernel,
        grid=(bsz, nt),
        in_specs=[tok_spec] + wspecs,
        out_specs=[tok_spec, state_spec(2), state_spec(3), state_spec(1)],
        out_shape=[jax.ShapeDtypeStruct((n_tok, D_MODEL), F32),
                   jax.ShapeDtypeStruct((bsz, 2, D_MODEL), F32),
                   jax.ShapeDtypeStruct((bsz, 3, D_MODEL), F32),
                   jax.ShapeDtypeStruct((bsz, 1, D_MODEL), F32)],
        scratch_shapes=[pltpu.VMEM((SUBLANES, D_MODEL), F32)] * 3,
        compiler_params=_params(2),
        name="mix_prompt",
    )(x, *wargs)


def _mix_sample(w, l, x, state_a, state_b, state_h):
    n_tok = x.shape[0]
    n_seq = n_tok // SUBLANES
    wargs, wspecs = _mix_weight_args(w, l)
    tok_spec = pl.BlockSpec((SB_MIX * SUBLANES, D_MODEL), lambda i: (i, 0))

    def state_in(rows):
        return pl.BlockSpec((None, SB_MIX, rows, D_MODEL), lambda i: (l, i, 0, 0))

    def state_out(rows):
        return pl.BlockSpec((SB_MIX, rows, D_MODEL), lambda i: (i, 0, 0))

    return pl.pallas_call(
        _mix_sample_kernel,
        grid=(n_seq // SB_MIX,),
        in_specs=[tok_spec, state_in(2), state_in(3), state_in(1)] + wspecs,
        out_specs=[tok_spec, state_out(2), state_out(3), state_out(1)],
        out_shape=[jax.ShapeDtypeStruct((n_tok, D_MODEL), F32),
                   jax.ShapeDtypeStruct((n_seq, 2, D_MODEL), F32),
                   jax.ShapeDtypeStruct((n_seq, 3, D_MODEL), F32),
                   jax.ShapeDtypeStruct((n_seq, 1, D_MODEL), F32)],
        compiler_params=_params(1),
        name="mix_sample",
    )(x, state_a, state_b, state_h, *wargs)


def _softmax_rows(s):
    e = jnp.exp(s - jnp.max(s, axis=-1, keepdims=True))
    return e / jnp.sum(e, axis=-1, keepdims=True)


def _attn_prompt_kernel(x_ref, k_ref, v_ref, g2_ref, g3_ref, wq_ref, wo_ref, y_ref):
    x = x_ref[...]
    q = _bdot(_rms(x, g2_ref[...]), wq_ref[...]).astype(BF16)
    heads = []
    for h in range(MEM_HEADS):
        sl = slice(h * MEM_HEAD_DIM, (h + 1) * MEM_HEAD_DIM)
        s = lax.dot_general(q[:, sl], k_ref[:, sl], (((1,), (1,)), ((), ())), preferred_element_type=F32)
        p = _softmax_rows(s * MEM_HEAD_DIM ** -0.5)
        heads.append(jnp.dot(p.astype(BF16), v_ref[:, sl], preferred_element_type=F32))
    o = jnp.concatenate(heads, axis=-1)
    y_ref[...] = x + _rms(_bdot(o, wo_ref[...]), g3_ref[...])


def _attn_prompt(w, l, x, mem_k, mem_v, bsz):
    n_tok = x.shape[0]
    nt = n_tok // bsz // TM_PROMPT
    sq = _layer_spec((None, D_MODEL, D_MODEL), l)
    tok_spec = pl.BlockSpec((TM_PROMPT, D_MODEL), lambda b, t: (b * nt + t, 0))
    mem_spec = pl.BlockSpec((None, N_MEM, D_MODEL), lambda b, t: (l, b, 0))
    return pl.pallas_call(
        _attn_prompt_kernel,
        grid=(bsz, nt),
        in_specs=[tok_spec, mem_spec, mem_spec, _gain_spec(l, 2), _gain_spec(l, 3), sq, sq],
        out_specs=tok_spec,
        out_shape=jax.ShapeDtypeStruct((n_tok, D_MODEL), F32),
        compiler_params=_params(2),
        name="attn_prompt",
    )(x, mem_k, mem_v, w["gains"], w["gains"], w["w_q"], w["w_o"])


def _attn_sample_kernel(x_ref, k_ref, v_ref, g2_ref, g3_ref, wq_ref, wo_ref, y_ref, q_scr, o_scr):
    i = pl.program_id(0)
    sb = k_ref.shape[0]
    rows = sb * SUBLANES

    @pl.when(i == 0)
    def _():
        q_scr[...] = _bdot(_rms(x_ref[...], g2_ref[...]), wq_ref[...]).astype(BF16)

    hq = MEM_HEADS * SUBLANES
    row_head = lax.broadcasted_iota(jnp.int32, (hq, N_MEM * MEM_HEADS), 0) // SUBLANES
    col_head = lax.broadcasted_iota(jnp.int32, (hq, N_MEM * MEM_HEADS), 1) % MEM_HEADS
    same_head = row_head == col_head
    start = pl.multiple_of(i * rows, rows)
    q = q_scr[pl.ds(start, rows), :]
    for b in range(sb):
        qb = q[b * SUBLANES:(b + 1) * SUBLANES, :]
        qs = jnp.concatenate([qb[:, h * MEM_HEAD_DIM:(h + 1) * MEM_HEAD_DIM] for h in range(MEM_HEADS)], axis=0)
        k2 = k_ref[b].reshape(N_MEM * MEM_HEADS, MEM_HEAD_DIM).astype(BF16)
        v2 = v_ref[b].reshape(N_MEM * MEM_HEADS, MEM_HEAD_DIM).astype(BF16)
        s = lax.dot_general(qs, k2, (((1,), (1,)), ((), ())), preferred_element_type=F32)
        p = _softmax_rows(jnp.where(same_head, s * MEM_HEAD_DIM ** -0.5, -jnp.inf))
        o = jnp.dot(p.astype(BF16), v2, preferred_element_type=F32)
        o_scr[pl.ds(start + b * SUBLANES, SUBLANES), :] = jnp.concatenate(
            [o[h * SUBLANES:(h + 1) * SUBLANES, :] for h in range(MEM_HEADS)], axis=1)

    @pl.when(i == pl.num_programs(0) - 1)
    def _():
        y_ref[...] = x_ref[...] + _rms(_bdot(o_scr[...], wo_ref[...]), g3_ref[...])


def _attn_sample(w, l, x, cache_k, cache_v):
    n_tok = x.shape[0]
    n_seq = n_tok // SUBLANES
    sq = _layer_spec((None, D_MODEL, D_MODEL), l)
    all_spec = pl.BlockSpec((n_tok, D_MODEL), lambda i: (0, 0), pipeline_mode=pl.Buffered(1))
    mem_spec = pl.BlockSpec((None, SB_ATTN, N_MEM, MEM_HEADS, MEM_HEAD_DIM), lambda i: (l, i, 0, 0, 0))
    return pl.pallas_call(
        _attn_sample_kernel,
        grid=(n_seq // SB_ATTN,),
        in_specs=[all_spec, mem_spec, mem_spec, _gain_spec(l, 2), _gain_spec(l, 3), sq, sq],
        out_specs=pl.BlockSpec((n_tok, D_MODEL), lambda i: (0, 0)),
        out_shape=jax.ShapeDtypeStruct((n_tok, D_MODEL), F32),
        scratch_shapes=[pltpu.VMEM((n_tok, D_MODEL), BF16), pltpu.VMEM((n_tok, D_MODEL), F32)],
        compiler_params=_params(1),
        name="attn_sample",
    )(x, cache_k, cache_v, w["gains"], w["gains"], w["w_q"], w["w_o"])


def _ffn_kernel(x_ref, g4_ref, g5_ref, wi_ref, wo_ref, y_ref):
    x = x_ref[...]
    xnb = _rms(x, g4_ref[...]).astype(BF16)
    gate = jnp.dot(xnb, wi_ref[:, :D_FF], preferred_element_type=F32)
    up = jnp.dot(xnb, wi_ref[:, D_FF:], preferred_element_type=F32)
    y = _bdot(jax.nn.silu(gate) * up, wo_ref[...])
    y_ref[...] = x + _rms(y, g5_ref[...])


def _ffn(w, l, x):
    n_tok = x.shape[0]
    tok_spec = pl.BlockSpec((TM_FFN, D_MODEL), lambda i: (i, 0))
    return pl.pallas_call(
        _ffn_kernel,
        grid=(n_tok // TM_FFN,),
        in_specs=[tok_spec, _gain_spec(l, 4), _gain_spec(l, 5),
                  _layer_spec((None, D_MODEL, 2 * D_FF), l), _layer_spec((None, D_FF, D_MODEL), l)],
        out_specs=tok_spec,
        out_shape=jax.ShapeDtypeStruct((n_tok, D_MODEL), F32),
        compiler_params=_params(1),
        name="ffn",
    )(x, w["gains"], w["gains"], w["w_ffn_in"], w["w_ffn_out"])


def _mem_kv_kernel(m_ref, g6_ref, wkv_ref, k_ref, v_ref, kb_ref, vb_ref):
    kv = _bdot(_rms(m_ref[...], g6_ref[...]), wkv_ref[...])
    k, v = kv[:, :D_MODEL], kv[:, D_MODEL:]
    k_ref[...] = k.reshape(k_ref.shape)
    v_ref[...] = v.reshape(v_ref.shape)
    kb_ref[...] = k.astype(BF16)
    vb_ref[...] = v.astype(BF16)


def _mem_kv(w, mem, depth):
    n_rows = mem.shape[0]
    out_spec = pl.BlockSpec((None, TM_MEM, MEM_HEADS, MEM_HEAD_DIM), lambda l, i: (l, i, 0, 0))
    out_shape = jax.ShapeDtypeStruct((depth, n_rows, MEM_HEADS, MEM_HEAD_DIM), F32)
    bf_spec = pl.BlockSpec((None, TM_MEM, D_MODEL), lambda l, i: (l, i, 0))
    bf_shape = jax.ShapeDtypeStruct((depth, n_rows, D_MODEL), BF16)
    return pl.pallas_call(
        _mem_kv_kernel,
        grid=(depth, n_rows // TM_MEM),
        in_specs=[pl.BlockSpec((TM_MEM, D_MODEL), lambda l, i: (i, 0)),
                  pl.BlockSpec((None, None, 1, D_MODEL), lambda l, i: (l, 6, 0, 0)),
                  pl.BlockSpec((None, D_MODEL, 2 * D_MODEL), lambda l, i: (l, 0, 0))],
        out_specs=[out_spec, out_spec, bf_spec, bf_spec],
        out_shape=[out_shape, out_shape, bf_shape, bf_shape],
        compiler_params=_params(2),
        name="mem_kv",
    )(mem, w["gains"], w["w_kv"])


def _block_diag_gate(wg):
    depth = wg.shape[0]
    w5 = wg.reshape(depth, GATE_CHUNKS, BLK_PER_CHUNK, BLK, BLK)
    eye = jnp.eye(BLK_PER_CHUNK, dtype=wg.dtype)
    dense = jnp.einsum("lcikm,ij->lcikjm", w5, eye)
    return dense.reshape(depth, GATE_CHUNKS, MXU_DIM, MXU_DIM).astype(BF16)


def kernel(x_prompt, x_sample, state_conv_a, state_conv_b, state_rglru, cache_mem_k, cache_mem_v, mem_prompt,
           norm_gains, w_in, conv_a_w, w_conv_out, conv_b_w, conv_b_b, w_gate_a, b_gate_a, w_gate_x, b_gate_x,
           lru_lambda, w_rnn_out, w_mix_out, w_kv_x, w_q_x, w_o_x, w_ffn_in, w_ffn_out):
    depth = norm_gains.shape[0]
    bsz, seq, _ = x_prompt.shape
    dec_bsz, dec_seq, _ = x_sample.shape
    assert dec_seq == SUBLANES and seq % TM_PROMPT == 0 and dec_bsz % SB_MIX == 0

    def row(v):
        return v.reshape(depth, 1, D_MODEL)

    w = dict(
        gains=norm_gains.reshape(depth, norm_gains.shape[1], 1, D_MODEL),
        w_in=w_in.astype(BF16), conv_a_w=conv_a_w, w_conv_out=w_conv_out.astype(BF16),
        conv_b_w=conv_b_w, conv_b_b=row(conv_b_b),
        w_gate_a=_block_diag_gate(w_gate_a), b_gate_a=row(b_gate_a),
        w_gate_x=_block_diag_gate(w_gate_x), b_gate_x=row(b_gate_x),
        lru_lambda=row(lru_lambda), w_rnn_out=w_rnn_out.astype(BF16), w_mix_out=w_mix_out.astype(BF16),
        w_kv=w_kv_x.astype(BF16), w_q=w_q_x.astype(BF16), w_o=w_o_x.astype(BF16),
        w_ffn_in=w_ffn_in.astype(BF16), w_ffn_out=w_ffn_out.astype(BF16),
    )

    xp = x_prompt.reshape(bsz * seq, D_MODEL)
    xs = x_sample.reshape(dec_bsz * dec_seq, D_MODEL)
    state_h = state_rglru.reshape(depth, dec_bsz, 1, D_MODEL)
    mem_k, mem_v, mem_kb, mem_vb = _mem_kv(w, mem_prompt.reshape(bsz * N_MEM, D_MODEL), depth)

    pa, pb, ph, sa, sb, sh = [], [], [], [], [], []
    for l in range(depth):
        xp, na, nb, nh = _mix_prompt(w, l, xp, bsz)
        pa.append(na); pb.append(nb); ph.append(nh)
        xp = _attn_prompt(w, l, xp, mem_kb, mem_vb, bsz)
        xp = _ffn(w, l, xp)

        xs, na, nb, nh = _mix_sample(w, l, xs, state_conv_a, state_conv_b, state_h)
        sa.append(na); sb.append(nb); sh.append(nh)
        xs = _attn_sample(w, l, xs, cache_mem_k, cache_mem_v)
        xs = _ffn(w, l, xs)

    mem_shape = (depth, bsz, N_MEM, MEM_HEADS, MEM_HEAD_DIM)
    return (xp.reshape(bsz, seq, D_MODEL), xs.reshape(dec_bsz, dec_seq, D_MODEL),
            jnp.stack(pa), jnp.stack(pb), jnp.stack(ph).reshape(depth, bsz, D_MODEL),
            mem_k.reshape(mem_shape), mem_v.reshape(mem_shape),
            jnp.stack(sa), jnp.stack(sb), jnp.stack(sh).reshape(depth, dec_bsz, D_MODEL))
```
